```python
import jax, jax.numpy as jnp
from jax import lax
import numpy as np

D_MODEL = 1024
BATCH = 32
SEQ = 2048
DEPTH = 4
DEC_BATCH = 16
DEC_SEQ = 16
PAST_LEN = 4096

CHUNK = 64
N_MIXERS = 4
BRANCH = D_MODEL
EPS = 1e-6
POOL_WINDOWS = (2, 4, 8, 16)
N_POOL_GROUPS = len(POOL_WINDOWS)
POOL_GROUP = BRANCH // N_POOL_GROUPS
POOL_STATE = max(POOL_WINDOWS) - 1
CONV_WIDTH = 31
CONV_STATE = CONV_WIDTH - 1
SB_HEADS = 8
SB_HEAD_DIM = BRANCH // SB_HEADS
Q_BLOCK = 128
MLP_CHUNK = 128
MLP_GROUPS = 4
MLP_GROUP = BRANCH // MLP_GROUPS
N_POOL_LAYERS = (DEPTH + 3) // 4
N_CONV_LAYERS = (DEPTH + 2) // 4
N_SB_LAYERS = (DEPTH + 1) // 4
N_GMLP_LAYERS = DEPTH // 4

kernel_name = "hybrid_streaming_encoder_step"


def _rmsnorm(x, g):
    x32 = x.astype(jnp.float32)
    y = x32 * lax.rsqrt(jnp.mean(x32 * x32, axis=-1, keepdims=True) + EPS)
    return (y * g.astype(jnp.float32)).astype(x.dtype)


def _layernorm(x, g, b):
    x32 = x.astype(jnp.float32)
    mu = jnp.mean(x32, axis=-1, keepdims=True)
    xc = x32 - mu
    var = jnp.mean(xc * xc, axis=-1, keepdims=True)
    y = xc * lax.rsqrt(var + EPS) * g.astype(jnp.float32) + b.astype(jnp.float32)
    return y.astype(x.dtype)


def _pool_mixer(u, buf, start_pos, w_grp, scale):
    bsz, t_len, _ = u.shape
    full = jnp.concatenate([buf.astype(u.dtype), u], axis=1)
    csum = jnp.cumsum(full.astype(jnp.float32), axis=1)
    csum = jnp.pad(csum, ((0, 0), (1, 0), (0, 0)))
    pos = start_pos + jnp.arange(t_len)
    hi = csum[:, POOL_STATE + 1:]
    means = []
    for gi, w in enumerate(POOL_WINDOWS):
        sl = slice(gi * POOL_GROUP, (gi + 1) * POOL_GROUP)
        lo = csum[:, POOL_STATE + 1 - w: POOL_STATE + 1 - w + t_len, sl]
        cnt = jnp.minimum(pos + 1, w).astype(jnp.float32)[None, :, None]
        means.append((hi[..., sl] - lo) / cnt)
    pooled = (jnp.concatenate(means, axis=-1) - u.astype(jnp.float32)).astype(u.dtype)
    pooled = pooled.reshape(bsz, t_len, N_POOL_GROUPS, POOL_GROUP)
    y = jnp.einsum('btgc,gcd->btgd', pooled, w_grp).reshape(bsz, t_len, BRANCH)
    return y * scale, full[:, -POOL_STATE:]


def _conv_module(a, b, buf, w_dw, b_dw, ln_g, ln_b):
    h = a * jax.nn.sigmoid(b)
    full = jnp.concatenate([buf.astype(h.dtype), h], axis=1)
    y = lax.conv_general_dilated(full, w_dw[:, None, :].astype(h.dtype), (1,), 'VALID',
                                 dimension_numbers=('NWC', 'WIO', 'NWC'),
                                 feature_group_count=BRANCH) + b_dw
    y = jax.nn.silu(_layernorm(y, ln_g, ln_b))
    return y, full[:, -CONV_STATE:]


def _sb_block(q, k, v, q_pos, k_pos):
    z = jnp.einsum('bqhd,bkhd->bhqk', q, k).astype(jnp.float32) * (SB_HEAD_DIM ** -0.5)
    mask = (k_pos[None, :] < q_pos[:, None])[None, None]
    log_beta = jax.nn.log_sigmoid(z)
    log_1m = jnp.where(mask, jax.nn.log_sigmoid(-z), 0.0)
    suffix = jnp.pad(lax.cumsum(log_1m, axis=3, reverse=True)[..., 1:],
                     ((0, 0), (0, 0), (0, 0), (0, 1)))
    attn = jnp.where(mask, jnp.exp(log_beta + suffix), 0.0)
    return jnp.einsum('bhqk,bkhd->bqhd', attn.astype(v.dtype), v)


def _sb_attend(q, k, v, q_offset):
    t_len = q.shape[1]
    outs = []
    for s0 in range(0, t_len, Q_BLOCK):
        s1 = min(s0 + Q_BLOCK, t_len)
        n_keys = q_offset + s1
        outs.append(_sb_block(q[:, s0:s1], k[:, :n_keys], v[:, :n_keys],
                              q_offset + jnp.arange(s0, s1), jnp.arange(n_keys)))
    return jnp.concatenate(outs, axis=1)


def _sgu(u, v, w_s, b_s):
    bsz, t_len, _ = v.shape
    length = min(t_len, MLP_CHUNK)
    n_chunks = t_len // length
    mask = jnp.tril(jnp.ones((length, length), dtype=bool))
    w = jnp.where(mask, w_s[:, :length, :length], 0.0)
    v5 = v.reshape(bsz, n_chunks, length, MLP_GROUPS, MLP_GROUP)
    mixed = jnp.einsum('gts,bnsgc->bntgc', w, v5) + b_s[:, :length].T[None, None, :, :, None]
    return u * mixed.reshape(bsz, t_len, BRANCH)


def setup_inputs(seed: int = 0) -> dict:
    key = jax.random.key(seed)
    ks = iter(jax.random.split(key, 40))

    def nrm(shape, scale):
        return jax.random.normal(next(ks), shape, jnp.float32) * scale

    d, e = D_MODEL, BRANCH
    return {
        "x_prompt": nrm((BATCH, SEQ, d), 1.0),
        "x_sample": nrm((DEC_BATCH, DEC_SEQ, d), 1.0),
        "cache_pool": nrm((N_POOL_LAYERS, DEC_BATCH, POOL_STATE, e), 1.0),
        "cache_conv": nrm((N_CONV_LAYERS, DEC_BATCH, CONV_STATE, e), 0.5),
        "cache_sb_k": nrm((N_SB_LAYERS, DEC_BATCH, PAST_LEN, SB_HEADS, SB_HEAD_DIM), 1.0),
        "cache_sb_v": nrm((N_SB_LAYERS, DEC_BATCH, PAST_LEN, SB_HEADS, SB_HEAD_DIM), 1.0),
        "norm_g": 1.0 + nrm((DEPTH, d), 0.05),
        "final_g": 1.0 + nrm((d,), 0.05),
        "a_w_in": nrm((N_POOL_LAYERS, d, 2 * e), d ** -0.5),
        "a_w_grp": nrm((N_POOL_LAYERS, N_POOL_GROUPS, POOL_GROUP, POOL_GROUP), POOL_GROUP ** -0.5),
        "a_scale": 1.0 + nrm((N_POOL_LAYERS, e), 0.1),
        "a_w_out": nrm((N_POOL_LAYERS, e, d), e ** -0.5),
        "b_w_in": nrm((N_CONV_LAYERS, d, 3 * e), d ** -0.5),
        "b_w_dw": nrm((N_CONV_LAYERS, CONV_WIDTH, e), CONV_WIDTH ** -0.5),
        "b_b_dw": nrm((N_CONV_LAYERS, e), 0.02),
        "b_ln_g": 1.0 + nrm((N_CONV_LAYERS, e), 0.05),
        "b_ln_b": nrm((N_CONV_LAYERS, e), 0.02),
        "b_w_out": nrm((N_CONV_LAYERS, e, d), e ** -0.5),
        "c_w_in": nrm((N_SB_LAYERS, d, 4 * e), d ** -0.5),
        "c_q_g": 1.0 + nrm((N_SB_LAYERS, SB_HEAD_DIM), 0.05),
        "c_k_g": 1.0 + nrm((N_SB_LAYERS, SB_HEAD_DIM), 0.05),
        "c_w_out": nrm((N_SB_LAYERS, e, d), e ** -0.5),
        "d_w_in": nrm((N_GMLP_LAYERS, d, 3 * e), d ** -0.5),
        "d_v_g": 1.0 + nrm((N_GMLP_LAYERS, e), 0.05),
        "d_w_s": nrm((N_GMLP_LAYERS, MLP_GROUPS, MLP_CHUNK, MLP_CHUNK), MLP_CHUNK ** -0.5),
        "d_b_s": 1.0 + nrm((N_GMLP_LAYERS, MLP_GROUPS, MLP_CHUNK), 0.05),
        "d_w_out": nrm((N_GMLP_LAYERS, e, d), e ** -0.5),
    }


def reference(x_prompt, x_sample, cache_pool, cache_conv, cache_sb_k, cache_sb_v,
              norm_g, final_g,
              a_w_in, a_w_grp, a_scale, a_w_out,
              b_w_in, b_w_dw, b_b_dw, b_ln_g, b_ln_b, b_w_out,
              c_w_in, c_q_g, c_k_g, c_w_out,
              d_w_in, d_v_g, d_w_s, d_b_s, d_w_out):

    def run(x, pool_bufs, conv_bufs, sb_ks, sb_vs, start):
        bsz, t_len, _ = x.shape
        new_pool, new_conv, new_k, new_v, new_gv = [], [], [], [], []
        for i in range(DEPTH):
            kind, j = i % N_MIXERS, i // N_MIXERS
            h = _rmsnorm(x, norm_g[i])
            if kind == 0:
                u, gate = jnp.split(h @ a_w_in[j], 2, axis=-1)
                y, buf = _pool_mixer(u, pool_bufs[j], start, a_w_grp[j], a_scale[j])
                new_pool.append(buf)
                w_out = a_w_out[j]
            elif kind == 1:
                ga, gb, gate = jnp.split(h @ b_w_in[j], 3, axis=-1)
                y, buf = _conv_module(ga, gb, conv_bufs[j], b_w_dw[j], b_b_dw[j], b_ln_g[j], b_ln_b[j])
                new_conv.append(buf)
                w_out = b_w_out[j]
            elif kind == 2:
                q, k, v, gate = jnp.split(h @ c_w_in[j], 4, axis=-1)
                hs = (bsz, t_len, SB_HEADS, SB_HEAD_DIM)
                q = _rmsnorm(q.reshape(hs), c_q_g[j])
                k = _rmsnorm(k.reshape(hs), c_k_g[j])
                v = v.reshape(hs)
                if sb_ks is None:
                    k_all, v_all = k, v
                else:
                    k_all = jnp.concatenate([sb_ks[j].astype(k.dtype), k], axis=1)
                    v_all = jnp.concatenate([sb_vs[j].astype(v.dtype), v], axis=1)
                y = _sb_attend(q, k_all, v_all, start).reshape(bsz, t_len, BRANCH)
                new_k.append(k)
                new_v.append(v)
                w_out = c_w_out[j]
            else:
                u, vv, gate = jnp.split(h @ d_w_in[j], 3, axis=-1)
                vv = _rmsnorm(vv, d_v_g[j])
                y = _sgu(u, vv, d_w_s[j], d_b_s[j])
                new_gv.append(vv)
                w_out = d_w_out[j]
            x = x + (y * jax.nn.silu(gate)) @ w_out
        return _rmsnorm(x, final_g), new_pool, new_conv, new_k, new_v, new_gv

    zero_pool = jnp.zeros((N_POOL_LAYERS, x_prompt.shape[0], POOL_STATE, BRANCH), x_prompt.dtype)
    zero_conv = jnp.zeros((N_CONV_LAYERS, x_prompt.shape[0], CONV_STATE, BRANCH), x_prompt.dtype)
    y_prompt, p_pool, p_conv, p_k, p_v, _ = run(x_prompt, zero_pool, zero_conv, None, None, 0)
    y_sample, s_pool, s_conv, s_k, s_v, s_gv = run(x_sample, cache_pool, cache_conv,
                                                  cache_sb_k, cache_sb_v, PAST_LEN)
    return (y_prompt, y_sample,
            jnp.stack(p_pool), jnp.stack(s_pool),
            jnp.stack(p_conv), jnp.stack(s_conv),
            jnp.stack(p_k), jnp.stack(p_v),
            jnp.stack(s_k), jnp.stack(s_v),
            jnp.stack(s_gv))
```

```python
import functools
import math

import jax
import jax.numpy as jnp
from jax import lax
from jax.experimental import pallas as pl
from jax.experimental.pallas import tpu as pltpu

F32 = jnp.float32
BF16 = jnp.bfloat16

D_MODEL = 1024
BRANCH = 1024
EPS = 1e-6
POOL_WINDOWS = (2, 4, 8, 16)
POOL_GROUP = BRANCH // len(POOL_WINDOWS)
POOL_STATE = 15
CONV_WIDTH = 31
CONV_STATE = 30
SB_HEADS = 8
SB_HEAD_DIM = 128
MLP_CHUNK = 128
MLP_GROUPS = 4
MLP_GROUP = BRANCH // MLP_GROUPS
HALO = 32
LOG2E = 1.4426950408889634
LN2 = 0.6931471805599453
Q_SCALE = -LOG2E * SB_HEAD_DIM ** -0.5
VMEM_LIMIT = 56 * 1024 * 1024


def _const_spec(shape):
    nd = len(shape)
    return pl.BlockSpec(shape, lambda *_: (0,) * nd)


def _params(n_axes):
    return pltpu.CompilerParams(
        dimension_semantics=("arbitrary",) * n_axes,
        vmem_limit_bytes=VMEM_LIMIT)


def _rms(x, g):
    ms = jnp.mean(x * x, axis=-1, keepdims=True)
    return x * lax.rsqrt(ms + EPS) * g


def _sigmoid(x):
    return 1.0 / (1.0 + jnp.exp(-x))


def _silu(x):
    return x * _sigmoid(x)


def _mm(a, b):
    return jnp.dot(a, b, preferred_element_type=F32)


def _pool_body(*refs, nb, tb, start_pos, has_buf):
    if has_buf:
        (x_ref, buf_ref, g_ref, win_ref, wgrp_ref, sc_ref, wout_ref,
         xo_ref, st_ref, ext_ref, s2_ref, s4_ref, s8_ref) = refs
    else:
        (x_ref, g_ref, win_ref, wgrp_ref, sc_ref, wout_ref,
         xo_ref, st_ref, ext_ref, s2_ref, s4_ref, s8_ref) = refs
        buf_ref = None
    t = pl.program_id(1)
    rows = nb * tb
    e, pg = BRANCH, POOL_GROUP

    @pl.when(t == 0)
    def _init():
        ext_ref[:, 0:HALO, :] = jnp.zeros((nb, HALO, e), F32)
        if buf_ref is not None:
            ext_ref[:, HALO - POOL_STATE:HALO, :] = buf_ref[0]
        s2_ref[:, 0:16, :] = jnp.zeros((nb, 16, e), F32)
        s4_ref[:, 0:16, :] = jnp.zeros((nb, 16, e - pg), F32)
        s8_ref[:, 0:16, :] = jnp.zeros((nb, 16, e - 2 * pg), F32)

    x = x_ref[...].reshape(rows, D_MODEL)
    h = _rms(x, g_ref[...]).astype(BF16)
    u = _mm(h, win_ref[:, 0:e])
    gate = _mm(h, win_ref[:, e:2 * e])
    ext_ref[:, HALO:HALO + tb, :] = u.reshape(nb, tb, e)

    n = tb + 16
    s2_ref[:, 16:16 + n, :] = ext_ref[:, 16:16 + n, :] + ext_ref[:, 15:15 + n, :]
    s4_ref[:, 16:16 + n, :] = s2_ref[:, 16:16 + n, pg:] + s2_ref[:, 14:14 + n, pg:]
    s8_ref[:, 16:16 + n, :] = s4_ref[:, 16:16 + n, pg:] + s4_ref[:, 12:12 + n, pg:]
    s16 = s8_ref[:, HALO:HALO + tb, pg:] + s8_ref[:, HALO - 8:HALO - 8 + tb, pg:]
    sums = (s2_ref[:, HALO:HALO + tb, 0:pg], s4_ref[:, HALO:HALO + tb, 0:pg],
            s8_ref[:, HALO:HALO + tb, 0:pg], s16)

    pos = start_pos + t * tb + lax.broadcasted_iota(jnp.int32, (1, tb, 1), 1)
    ys = []
    for gi, w in enumerate(POOL_WINDOWS):
        inv_cnt = 1.0 / jnp.minimum(pos + 1, w).astype(F32)
        u_g = ext_ref[:, HALO:HALO + tb, gi * pg:(gi + 1) * pg]
        pooled = (sums[gi] * inv_cnt - u_g).reshape(rows, pg).astype(BF16)
        ys.append(_mm(pooled, wgrp_ref[gi]))
    y = jnp.concatenate(ys, axis=-1) * sc_ref[...]
    y = (y * _silu(gate)).astype(BF16)
    xo_ref[...] = (x + _mm(y, wout_ref[...])).reshape(nb, tb, D_MODEL)

    st_ref[0] = ext_ref[:, HALO + tb - POOL_STATE:HALO + tb, :]
    ext_ref[:, 16:HALO, :] = ext_ref[:, 16 + tb:HALO + tb, :]


def _pool_layer(x, buf, g, w_in, w_grp, scale, w_out, *, nb, tb, start_pos):
    bsz, t_len, d = x.shape
    e, pg = BRANCH, POOL_GROUP
    has_buf = buf is not None
    grid = (bsz // nb, t_len // tb)
    xspec = pl.BlockSpec((nb, tb, d), lambda b, t: (b, t, 0))
    stspec = pl.BlockSpec((1, nb, POOL_STATE, e), lambda b, t: (0, b, 0, 0))
    in_specs = [xspec]
    args = [x]
    if has_buf:
        in_specs.append(stspec)
        args.append(buf)
    in_specs += [_const_spec((1, d)), _const_spec((d, 2 * e)), _const_spec((4, pg, pg)),
                 _const_spec((1, e)), _const_spec((e, d))]
    args += [g.reshape(1, d), w_in.astype(BF16), w_grp.astype(BF16),
             scale.reshape(1, e), w_out.astype(BF16)]
    return pl.pallas_call(
        functools.partial(_pool_body, nb=nb, tb=tb, start_pos=start_pos, has_buf=has_buf),
        grid=grid,
        in_specs=in_specs,
        out_specs=[xspec, stspec],
        out_shape=[jax.ShapeDtypeStruct(x.shape, F32),
                   jax.ShapeDtypeStruct((1, bsz, POOL_STATE, e), F32)],
        scratch_shapes=[pltpu.VMEM((nb, HALO + tb, e), F32),
                        pltpu.VMEM((nb, HALO + tb, e), F32),
                        pltpu.VMEM((nb, HALO + tb, e - pg), F32),
                        pltpu.VMEM((nb, HALO + tb, e - 2 * pg), F32)],
        compiler_params=_params(2),
        name="pool_layer",
    )(*args)


def _conv_body(*refs, nb, tb, has_buf):
    if has_buf:
        (x_ref, buf_ref, g_ref, win_ref, wdw_ref, bdw_ref, lng_ref, lnb_ref, wout_ref,
         xo_ref, st_ref, ext_ref) = refs
    else:
        (x_ref, g_ref, win_ref, wdw_ref, bdw_ref, lng_ref, lnb_ref, wout_ref,
         xo_ref, st_ref, ext_ref) = refs
        buf_ref = None
    t = pl.program_id(1)
    rows = nb * tb
    e = BRANCH

    @pl.when(t == 0)
    def _init():
        ext_ref[:, 0:HALO, :] = jnp.zeros((nb, HALO, e), F32)
        if buf_ref is not None:
            ext_ref[:, HALO - CONV_STATE:HALO, :] = buf_ref[0]

    x = x_ref[...].reshape(rows, D_MODEL)
    h = _rms(x, g_ref[...]).astype(BF16)
    ga = _mm(h, win_ref[:, 0:e])
    gb = _mm(h, win_ref[:, e:2 * e])
    gate = _mm(h, win_ref[:, 2 * e:3 * e])
    ext_ref[:, HALO:HALO + tb, :] = (ga * _sigmoid(gb)).reshape(nb, tb, e)

    off = HALO - CONV_STATE
    acc = jnp.zeros((nb, tb, e), F32) + bdw_ref[...]
    for k in range(CONV_WIDTH):
        acc = acc + ext_ref[:, off + k:off + k + tb, :] * wdw_ref[k:k + 1, :]
    acc = acc.reshape(rows, e)
    mu = jnp.mean(acc, axis=-1, keepdims=True)
    xc = acc - mu
    var = jnp.mean(xc * xc, axis=-1, keepdims=True)
    y = _silu(xc * lax.rsqrt(var + EPS) * lng_ref[...] + lnb_ref[...])
    y = (y * _silu(gate)).astype(BF16)
    xo_ref[...] = (x + _mm(y, wout_ref[...])).reshape(nb, tb, D_MODEL)

    st_ref[0] = ext_ref[:, HALO + tb - CONV_STATE:HALO + tb, :]
    tail = ext_ref[:, tb:HALO + tb, :]
    ext_ref[:, 0:HALO, :] = tail


def _conv_layer(x, buf, g, w_in, w_dw, b_dw, ln_g, ln_b, w_out, *, nb, tb):
    bsz, t_len, d = x.shape
    e = BRANCH
    has_buf = buf is not None
    grid = (bsz // nb, t_len // tb)
    xspec = pl.BlockSpec((nb, tb, d), lambda b, t: (b, t, 0))
    stspec = pl.BlockSpec((1, nb, CONV_STATE, e), lambda b, t: (0, b, 0, 0))
    in_specs = [xspec]
    args = [x]
    if has_buf:
        in_specs.append(stspec)
        args.append(buf)
    in_specs += [_const_spec((1, d)), _const_spec((d, 3 * e)), _const_spec((CONV_WIDTH, e)),
                 _const_spec((1, e)), _const_spec((1, e)), _const_spec((1, e)),
                 _const_spec((e, d))]
    args += [g.reshape(1, d), w_in.astype(BF16), w_dw, b_dw.reshape(1, e),
             ln_g.reshape(1, e), ln_b.reshape(1, e), w_out.astype(BF16)]
    return pl.pallas_call(
        functools.partial(_conv_body, nb=nb, tb=tb, has_buf=has_buf),
        grid=grid,
        in_specs=in_specs,
        out_specs=[xspec, stspec],
        out_shape=[jax.ShapeDtypeStruct(x.shape, F32),
                   jax.ShapeDtypeStruct((1, bsz, CONV_STATE, e), F32)],
        scratch_shapes=[pltpu.VMEM((nb, HALO + tb, e), F32)],
        compiler_params=_params(2),
        name="conv_layer",
    )(*args)


def _head_rms(x, g):
    outs = []
    for hd in range(SB_HEADS):
        xh = x[:, hd * SB_HEAD_DIM:(hd + 1) * SB_HEAD_DIM]
        outs.append(_rms(xh, g))
    return outs


def _sbproj_body(*refs, with_bf16_kv):
    if with_bf16_kv:
        (x_ref, g_ref, win_ref, qg_ref, kg_ref,
         q_ref, k_ref, v_ref, sg_ref, kb_ref, vb_ref) = refs
    else:
        (x_ref, g_ref, win_ref, qg_ref, kg_ref,
         q_ref, k_ref, v_ref, sg_ref) = refs
    e, dh = BRANCH, SB_HEAD_DIM
    x = x_ref[...]
    h = _rms(x, g_ref[...]).astype(BF16)
    q = _mm(h, win_ref[:, 0:e])
    for hd, qn in enumerate(_head_rms(q, qg_ref[...])):
        q_ref[:, hd * dh:(hd + 1) * dh] = (qn * Q_SCALE).astype(BF16)
    k = _mm(h, win_ref[:, e:2 * e])
    for hd, kn in enumerate(_head_rms(k, kg_ref[...])):
        k_ref[:, hd * dh:(hd + 1) * dh] = kn
        if with_bf16_kv:
            kb_ref[:, hd * dh:(hd + 1) * dh] = kn.astype(BF16)
    v = _mm(h, win_ref[:, 2 * e:3 * e])
    v_ref[...] = v
    if with_bf16_kv:
        vb_ref[...] = v.astype(BF16)
    sg_ref[...] = _silu(_mm(h, win_ref[:, 3 * e:4 * e]))


def _sb_proj(x2d, g, w_in, q_g, k_g, *, rows, with_bf16_kv):
    n_tok, d = x2d.shape
    e = BRANCH
    rspec = pl.BlockSpec((rows, e), lambda i: (i, 0))
    out_shape = [jax.ShapeDtypeStruct((n_tok, e), BF16),
                 jax.ShapeDtypeStruct((n_tok, e), F32),
                 jax.ShapeDtypeStruct((n_tok, e), F32),
                 jax.ShapeDtypeStruct((n_tok, e), F32)]
    if with_bf16_kv:
        out_shape += [jax.ShapeDtypeStruct((n_tok, e), BF16)] * 2
    return pl.pallas_call(
        functools.partial(_sbproj_body, with_bf16_kv=with_bf16_kv),
        grid=(n_tok // rows,),
        in_specs=[rspec, _const_spec((1, d)), _const_spec((d, 4 * e)),
                  _const_spec((1, SB_HEAD_DIM)), _const_spec((1, SB_HEAD_DIM))],
        out_specs=[rspec] * len(out_shape),
        out_shape=out_shape,
        compiler_params=_params(1),
        name="sb_proj",
    )(x2d, g.reshape(1, d), w_in.astype(BF16), q_g.reshape(1, -1), k_g.reshape(1, -1))


def _log2_one_minus_beta(n):
    soft = jnp.log(1.0 + jnp.exp2(-jnp.abs(n))) * LOG2E
    return jnp.minimum(n, 0.0) - soft


def _sb_block(qh, kb, vb, c, acc, trio, mask):
    tk = kb.shape[0]
    n = lax.dot_general(qh, kb, (((1,), (1,)), ((), ())), preferred_element_type=F32)
    l1m = _log2_one_minus_beta(n)
    if mask is not None:
        l1m = jnp.where(mask, l1m, 0.0)
    sx = _mm(l1m.astype(BF16), trio)
    s = sx[:, 0:tk] + jnp.concatenate([c] * (tk // 128), axis=1)
    a = jnp.exp2(s - n)
    if mask is not None:
        a = jnp.where(mask, a, 0.0)
    acc = acc + _mm(a.astype(BF16), vb)
    c = c + sx[:, tk:tk + 128]
    return c, acc


def _sbattn_body(q_ref, kb_ref, vb_ref, sg_ref, x_ref, wout_ref, xo_ref, y_ref, *, tq):
    i = pl.program_id(1)
    tk = tq
    dh = SB_HEAD_DIM
    row = lax.broadcasted_iota(jnp.int32, (tq, tk), 0)
    col = lax.broadcasted_iota(jnp.int32, (tq, tk), 1)
    mask = col < row
    tr = lax.broadcasted_iota(jnp.int32, (tk, tk + 128), 0)
    tc = lax.broadcasted_iota(jnp.int32, (tk, tk + 128), 1)
    trio = jnp.where((tr >= tc) | (tc >= tk), 1.0, 0.0).astype(BF16)
    r0 = pl.multiple_of(i * tq, tq)
    zero = jnp.zeros((tq, 128), F32)
    for hd in range(SB_HEADS):
        sl = slice(hd * dh, (hd + 1) * dh)
        qh = q_ref[0, :, sl]
        c, acc = _sb_block(qh, kb_ref[0, pl.ds(r0, tk), sl], vb_ref[0, pl.ds(r0, tk), sl],
                           zero, zero, trio, mask)

        def body(jj, carry, qh=qh, sl=sl):
            s0 = pl.multiple_of((i - 1 - jj) * tk, tk)
            return _sb_block(qh, kb_ref[0, pl.ds(s0, tk), sl], vb_ref[0, pl.ds(s0, tk), sl],
                             carry[0], carry[1], trio, None)

        c, acc = lax.fori_loop(0, i, body, (c, acc))
        y_ref[:, sl] = acc
    y = (y_ref[...] * sg_ref[0]).astype(BF16)
    xo_ref[0] = x_ref[0] + _mm(y, wout_ref[...])


def _sb_attn_prompt(q, kb, vb, sg, x, w_out, *, tq):
    bsz, t_len, e = q.shape
    d = x.shape[-1]
    tspec = pl.BlockSpec((1, tq, e), lambda b, i: (b, i, 0))
    kvspec = pl.BlockSpec((1, t_len, e), lambda b, i: (b, 0, 0))
    return pl.pallas_call(
        functools.partial(_sbattn_body, tq=tq),
        grid=(bsz, t_len // tq),
        in_specs=[tspec, kvspec, kvspec, tspec, tspec, _const_spec((e, d))],
        out_specs=tspec,
        out_shape=jax.ShapeDtypeStruct(x.shape, F32),
        scratch_shapes=[pltpu.VMEM((tq, e), F32)],
        compiler_params=_params(2),
        name="sb_attn_prompt",
    )(q, kb, vb, sg, x, w_out.astype(BF16))


def _sbattn_sample_body(q_ref, kn_ref, vn_ref, kc_ref, vc_ref, sg_ref, x_ref, wout_ref,
                        xo_ref, qbd_ref, c_ref, acc_ref, *, t_new, tkc, n_steps):
    s = pl.program_id(1)
    e, dh = BRANCH, SB_HEAD_DIM
    lanes = SB_HEADS * t_new

    def block(kblk, vblk, mask):
        tk = kblk.shape[0]
        n = lax.dot_general(kblk, qbd_ref[...], (((1,), (1,)), ((), ())),
                            preferred_element_type=F32)
        l1m = _log2_one_minus_beta(n)
        if mask is not None:
            l1m = jnp.where(mask, l1m, 0.0)
        tr = lax.broadcasted_iota(jnp.int32, (tk, tk), 0)
        tc = lax.broadcasted_iota(jnp.int32, (tk, tk), 1)
        tri = jnp.where(tc >= tr, 1.0, 0.0).astype(BF16)
        sfx = _mm(tri, l1m.astype(BF16)) + c_ref[0:1, :]
        a = jnp.exp2(sfx - n)
        if mask is not None:
            a = jnp.where(mask, a, 0.0)
        acc_ref[...] += _mm(a.T.astype(BF16), vblk)
        c_ref[...] = c_ref[...] + jnp.sum(l1m, axis=0, keepdims=True)

    @pl.when(s == 0)
    def _first():
        q8 = jnp.concatenate([q_ref[0]] * SB_HEADS, axis=0)
        rh = lax.broadcasted_iota(jnp.int32, (lanes, e), 0) // t_new
        ch = lax.broadcasted_iota(jnp.int32, (lanes, e), 1) // dh
        qbd_ref[...] = jnp.where(rh == ch, q8, jnp.zeros_like(q8))
        c_ref[...] = jnp.zeros_like(c_ref)
        acc_ref[...] = jnp.zeros_like(acc_ref)
        pad = jnp.zeros((128 - t_new, e), F32)
        kblk = jnp.concatenate([kn_ref[0], pad], axis=0).astype(BF16)
        vblk = jnp.concatenate([vn_ref[0], pad], axis=0).astype(BF16)
        key = lax.broadcasted_iota(jnp.int32, (128, lanes), 0)
        qry = lax.broadcasted_iota(jnp.int32, (128, lanes), 1) % t_new
        block(kblk, vblk, key < qry)

    @pl.when(s > 0)
    def _cached():
        block(kc_ref[0, 0].astype(BF16), vc_ref[0, 0].astype(BF16), None)

    @pl.when(s == n_steps - 1)
    def _last():
        parts = [acc_ref[hd * t_new:(hd + 1) * t_new, hd * dh:(hd + 1) * dh]
                 for hd in range(SB_HEADS)]
        y = (jnp.concatenate(parts, axis=1) * sg_ref[0]).astype(BF16)
        xo_ref[0] = x_ref[0] + _mm(y, wout_ref[...])


def _sb_attn_sample(q, k_new, v_new, cache_k, cache_v, sg, x, w_out, *, tkc):
    bsz, t_new, e = q.shape
    d = x.shape[-1]
    past = cache_k.shape[2]
    n_kv = past // tkc
    n_steps = n_kv + 1
    lanes = SB_HEADS * t_new
    assert lanes == 128 and past % tkc == 0
    tspec = pl.BlockSpec((1, t_new, e), lambda b, s: (b, 0, 0))
    cspec = pl.BlockSpec((1, 1, tkc, e), lambda b, s: (0, b, n_kv - jnp.maximum(s, 1), 0))
    return pl.pallas_call(
        functools.partial(_sbattn_sample_body, t_new=t_new, tkc=tkc, n_steps=n_steps),
        grid=(bsz, n_steps),
        in_specs=[tspec, tspec, tspec, cspec, cspec, tspec, tspec, _const_spec((e, d))],
        out_specs=tspec,
        out_shape=jax.ShapeDtypeStruct(x.shape, F32),
        scratch_shapes=[pltpu.VMEM((lanes, e), BF16),
                        pltpu.VMEM((8, lanes), F32),
                        pltpu.VMEM((lanes, e), F32)],
        compiler_params=_params(2),
        name="sb_attn_sample",
    )(q, k_new, v_new, cache_k, cache_v, sg, x, w_out.astype(BF16))


def _gmlp_body(*refs, nb, tb, length, with_gv):
    if with_gv:
        (x_ref, g_ref, win_ref, vg_ref, ws_ref, bst_ref, wout_ref, fg_ref,
         yo_ref, gv_ref) = refs
    else:
        (x_ref, g_ref, win_ref, vg_ref, ws_ref, bst_ref, wout_ref, fg_ref, yo_ref) = refs
    rows = nb * tb
    e, mg = BRANCH, MLP_GROUP
    x = x_ref[...].reshape(rows, D_MODEL)
    h = _rms(x, g_ref[...]).astype(BF16)
    u = _mm(h, win_ref[:, 0:e])
    vv = _rms(_mm(h, win_ref[:, e:2 * e]), vg_ref[...])
    gate = _mm(h, win_ref[:, 2 * e:3 * e])
    if with_gv:
        gv_ref[0] = vv.reshape(nb, tb, e)
    vb = vv.astype(BF16)
    tr = lax.broadcasted_iota(jnp.int32, (length, length), 0)
    tc = lax.broadcasted_iota(jnp.int32, (length, length), 1)
    cols = []
    for gi in range(MLP_GROUPS):
        w = jnp.where(tr >= tc, ws_ref[gi, 0:length, 0:length], 0.0).astype(BF16)
        b = bst_ref[0:length, gi:gi + 1]
        chunks = []
        for r in range(rows // length):
            vch = vb[r * length:(r + 1) * length, gi * mg:(gi + 1) * mg]
            chunks.append(_mm(w, vch) + b)
        cols.append(jnp.concatenate(chunks, axis=0))
    y = u * jnp.concatenate(cols, axis=1)
    y = (y * _silu(gate)).astype(BF16)
    xo = x + _mm(y, wout_ref[...])
    yo_ref[...] = _rms(xo, fg_ref[...]).reshape(nb, tb, D_MODEL)


def _gmlp_layer(x, g, w_in, v_g, w_s, b_s, w_out, final_g, *, nb, tb, with_gv):
    bsz, t_len, d = x.shape
    e = BRANCH
    length = min(t_len, MLP_CHUNK)
    assert tb % length == 0
    xspec = pl.BlockSpec((nb, tb, d), lambda b, t: (b, t, 0))
    out_specs = [xspec]
    out_shape = [jax.ShapeDtypeStruct(x.shape, F32)]
    if with_gv:
        out_specs.append(pl.BlockSpec((1, nb, tb, e), lambda b, t: (0, b, t, 0)))
        out_shape.append(jax.ShapeDtypeStruct((1, bsz, t_len, e), F32))
    return pl.pallas_call(
        functools.partial(_gmlp_body, nb=nb, tb=tb, length=length, with_gv=with_gv),
        grid=(bsz // nb, t_len // tb),
        in_specs=[xspec, _const_spec((1, d)), _const_spec((d, 3 * e)), _const_spec((1, e)),
                  _const_spec((MLP_GROUPS, MLP_CHUNK, MLP_CHUNK)),
                  _const_spec((MLP_CHUNK, MLP_GROUPS)), _const_spec((e, d)),
                  _const_spec((1, d))],
        out_specs=out_specs,
        out_shape=out_shape,
        compiler_params=_params(2),
        name="gmlp_layer",
    )(x, g.reshape(1, d), w_in.astype(BF16), v_g.reshape(1, e), w_s, b_s.T,
      w_out.astype(BF16), final_g.reshape(1, d))


def kernel(x_prompt, x_sample, cache_pool, cache_conv, cache_sb_k, cache_sb_v, norm_g, final_g,
           a_w_in, a_w_grp, a_scale, a_w_out,
           b_w_in, b_w_dw, b_b_dw, b_ln_g, b_ln_b, b_w_out,
           c_w_in, c_q_g, c_k_g, c_w_out,
           d_w_in, d_v_g, d_w_s, d_b_s, d_w_out):
    assert norm_g.shape[0] == 4 and a_w_in.shape[0] == 1
    bp, tp, d = x_prompt.shape
    bs, ts, _ = x_sample.shape
    past = cache_sb_k.shape[2]
    e = BRANCH
    hs = (SB_HEADS, SB_HEAD_DIM)
    tbp = 512

    def run(x, pool_buf, conv_buf, sb_k, sb_v, start, nb, tb):
        bsz, t_len, _ = x.shape
        x, pool_st = _pool_layer(x, pool_buf, norm_g[0], a_w_in[0], a_w_grp[0], a_scale[0],
                                 a_w_out[0], nb=nb, tb=tb, start_pos=start)
        x, conv_st = _conv_layer(x, conv_buf, norm_g[1], b_w_in[0], b_w_dw[0], b_b_dw[0],
                                 b_ln_g[0], b_ln_b[0], b_w_out[0], nb=nb, tb=tb)
        prompt = sb_k is None
        proj = _sb_proj(x.reshape(bsz * t_len, d), norm_g[2], c_w_in[0], c_q_g[0], c_k_g[0],
                        rows=nb * tb, with_bf16_kv=prompt)
        q, k, v, sg = (a.reshape(bsz, t_len, e) for a in proj[:4])
        if prompt:
            kb, vb = (a.reshape(bsz, t_len, e) for a in proj[4:])
            x = _sb_attn_prompt(q, kb, vb, sg, x, c_w_out[0], tq=256)
        else:
            x = _sb_attn_sample(q, k, v, sb_k.reshape(1, bsz, past, e),
                                sb_v.reshape(1, bsz, past, e), sg, x, c_w_out[0], tkc=512)
        outs = _gmlp_layer(x, norm_g[3], d_w_in[0], d_v_g[0], d_w_s[0], d_b_s[0], d_w_out[0],
                           final_g, nb=nb, tb=tb, with_gv=not prompt)
        new_k = k.reshape(1, bsz, t_len, *hs)
        new_v = v.reshape(1, bsz, t_len, *hs)
        return outs, pool_st, conv_st, new_k, new_v

    (y_p,), p_pool, p_conv, p_k, p_v = run(x_prompt, None, None, None, None, 0, 1, tbp)
    (y_s, s_gv), s_pool, s_conv, s_k, s_v = run(x_sample, cache_pool, cache_conv,
                                                cache_sb_k, cache_sb_v, past, bs, ts)
    return (y_p, y_s, p_pool, s_pool, p_conv, s_conv, p_k, p_v, s_k, s_v, s_gv)
```

```python
import functools

import jax
import jax.numpy as jnp
from jax import lax
from jax.experimental import pallas as pl
from jax.experimental.pallas import tpu as pltpu

F32 = jnp.float32
BF16 = jnp.bfloat16

D_MODEL = 1024
BRANCH = 1024
EPS = 1e-6
LANES = 128
POOL_WINDOWS = (2, 4, 8, 16)
POOL_GROUP = BRANCH // len(POOL_WINDOWS)
POOL_STATE = 15
CONV_WIDTH = 31
CONV_STATE = 30
CONV_ROWS = 32
SB_HEADS = 8
SB_HEAD_DIM = 128
MLP_CHUNK = 128
MLP_GROUPS = 4
MLP_GROUP = BRANCH // MLP_GROUPS
HALO = 32
LOG2E = 1.4426950408889634
Q_SCALE = -LOG2E * SB_HEAD_DIM ** -0.5
VMEM_LIMIT = 56 * 1024 * 1024


def _const_spec(shape):
    nd = len(shape)
    return pl.BlockSpec(shape, lambda *_: (0,) * nd)


def _params(n_axes):
    return pltpu.CompilerParams(
        dimension_semantics=("arbitrary",) * n_axes,
        vmem_limit_bytes=VMEM_LIMIT)


def _rms(x, g):
    ms = jnp.mean(x * x, axis=-1, keepdims=True)
    return x * lax.rsqrt(ms + EPS) * g


def _sigmoid(x):
    return 1.0 / (1.0 + jnp.exp(-x))


def _silu(x):
    return x * _sigmoid(x)


def _mm(a, b):
    return jnp.dot(a, b, preferred_element_type=F32)


def _mm_nt(a, b):
    return lax.dot_general(a, b, (((1,), (1,)), ((), ())), preferred_element_type=F32)


def _pool_body(*refs, nb, tb, start_pos, has_buf):
    if has_buf:
        (x_ref, buf_ref, g_ref, win_ref, wgrp_ref, sc_ref, wout_ref,
         xo_ref, st_ref, ext_ref, s2_ref, s4_ref, s8_ref) = refs
    else:
        (x_ref, g_ref, win_ref, wgrp_ref, sc_ref, wout_ref,
         xo_ref, st_ref, ext_ref, s2_ref, s4_ref, s8_ref) = refs
        buf_ref = None
    t = pl.program_id(1)
    rows = nb * tb
    e, pg = BRANCH, POOL_GROUP

    @pl.when(t == 0)
    def _init():
        ext_ref[:, 0:HALO, :] = jnp.zeros((nb, HALO, e), F32)
        if buf_ref is not None:
            ext_ref[:, HALO - POOL_STATE:HALO, :] = buf_ref[0]
        s2_ref[:, 0:16, :] = jnp.zeros((nb, 16, e), F32)
        s4_ref[:, 0:16, :] = jnp.zeros((nb, 16, e - pg), F32)
        s8_ref[:, 0:16, :] = jnp.zeros((nb, 16, e - 2 * pg), F32)

    x = x_ref[...].reshape(rows, D_MODEL)
    h = _rms(x, g_ref[...]).astype(BF16)
    u = _mm(h, win_ref[:, 0:e])
    gate = _mm(h, win_ref[:, e:2 * e])
    ext_ref[:, HALO:HALO + tb, :] = u.reshape(nb, tb, e)

    n = tb + 16
    s2_ref[:, 16:16 + n, :] = ext_ref[:, 16:16 + n, :] + ext_ref[:, 15:15 + n, :]
    s4_ref[:, 16:16 + n, :] = s2_ref[:, 16:16 + n, pg:] + s2_ref[:, 14:14 + n, pg:]
    s8_ref[:, 16:16 + n, :] = s4_ref[:, 16:16 + n, pg:] + s4_ref[:, 12:12 + n, pg:]
    s16 = s8_ref[:, HALO:HALO + tb, pg:] + s8_ref[:, HALO - 8:HALO - 8 + tb, pg:]
    sums = (s2_ref[:, HALO:HALO + tb, 0:pg], s4_ref[:, HALO:HALO + tb, 0:pg],
            s8_ref[:, HALO:HALO + tb, 0:pg], s16)

    pos = start_pos + t * tb + lax.broadcasted_iota(jnp.int32, (1, tb, 1), 1)
    ys = []
    for gi, w in enumerate(POOL_WINDOWS):
        inv_cnt = 1.0 / jnp.minimum(pos + 1, w).astype(F32)
        u_g = ext_ref[:, HALO:HALO + tb, gi * pg:(gi + 1) * pg]
        pooled = (sums[gi] * inv_cnt - u_g).reshape(rows, pg).astype(BF16)
        ys.append(_mm(pooled, wgrp_ref[gi]))
    y = jnp.concatenate(ys, axis=-1) * sc_ref[...]
    y = (y * _silu(gate)).astype(BF16)
    xo_ref[...] = (x + _mm(y, wout_ref[...])).reshape(nb, tb, D_MODEL)

    st_ref[0] = ext_ref[:, HALO + tb - POOL_STATE:HALO + tb, :]
    ext_ref[:, 16:HALO, :] = ext_ref[:, 16 + tb:HALO + tb, :]


def _pool_layer(x, buf, g, w_in, w_grp, scale, w_out, *, nb, tb, start_pos):
    bsz, t_len, d = x.shape
    e, pg = BRANCH, POOL_GROUP
    has_buf = buf is not None
    grid = (bsz // nb, t_len // tb)
    xspec = pl.BlockSpec((nb, tb, d), lambda b, t: (b, t, 0))
    stspec = pl.BlockSpec((1, nb, POOL_STATE, e), lambda b, t: (0, b, 0, 0))
    in_specs = [xspec]
    args = [x]
    if has_buf:
        in_specs.append(stspec)
        args.append(buf)
    in_specs += [_const_spec((1, d)), _const_spec((d, 2 * e)), _const_spec((4, pg, pg)),
                 _const_spec((1, e)), _const_spec((e, d))]
    args += [g.reshape(1, d), w_in.astype(BF16), w_grp.astype(BF16),
             scale.reshape(1, e), w_out.astype(BF16)]
    return pl.pallas_call(
        functools.partial(_pool_body, nb=nb, tb=tb, start_pos=start_pos, has_buf=has_buf),
        grid=grid,
        in_specs=in_specs,
        out_specs=[xspec, stspec],
        out_shape=[jax.ShapeDtypeStruct(x.shape, F32),
                   jax.ShapeDtypeStruct((1, bsz, POOL_STATE, e), F32)],
        scratch_shapes=[pltpu.VMEM((nb, HALO + tb, e), F32),
                        pltpu.VMEM((nb, HALO + tb, e), F32),
                        pltpu.VMEM((nb, HALO + tb, e - pg), F32),
                        pltpu.VMEM((nb, HALO + tb, e - 2 * pg), F32)],
        compiler_params=_params(2),
        name="pool_layer",
    )(*args)


def _conv_body(*refs, nb, tb, has_buf):
    if has_buf:
        (x_ref, buf_ref, g_ref, win_ref, wdw_ref, bdw_ref, lng_ref, lnb_ref, wout_ref,
         xo_ref, st_ref, ext_ref, yc_ref) = refs
    else:
        (x_ref, g_ref, win_ref, wdw_ref, bdw_ref, lng_ref, lnb_ref, wout_ref,
         xo_ref, st_ref, ext_ref, yc_ref) = refs
        buf_ref = None
    t = pl.program_id(1)
    rows = nb * tb
    e = BRANCH
    n_lb = e // LANES
    rc = min(CONV_ROWS, tb)

    @pl.when(t == 0)
    def _init():
        for lb in range(n_lb):
            ext_ref[lb, :, 0:HALO, :] = jnp.zeros((nb, HALO, LANES), F32)
            if buf_ref is not None:
                ext_ref[lb, :, HALO - CONV_STATE:HALO, :] = (
                    buf_ref[0, :, :, lb * LANES:(lb + 1) * LANES])

    x = x_ref[...].reshape(rows, D_MODEL)
    h = _rms(x, g_ref[...]).astype(BF16)
    ga = _mm(h, win_ref[:, 0:e])
    gb = _mm(h, win_ref[:, e:2 * e])
    hh = ga * _sigmoid(gb)
    for lb in range(n_lb):
        ext_ref[lb, :, HALO:HALO + tb, :] = (
            hh[:, lb * LANES:(lb + 1) * LANES].reshape(nb, tb, LANES))

    off = HALO - CONV_STATE

    def lane_block(lb, carry):
        taps = [jnp.broadcast_to(wdw_ref[lb, k:k + 1, :], (rc, LANES))
                for k in range(CONV_WIDTH)]
        bias = jnp.broadcast_to(bdw_ref[lb], (rc, LANES))
        for b in range(nb):
            for c0 in range(0, tb, rc):
                acc = bias
                for k in range(CONV_WIDTH):
                    acc = acc + ext_ref[lb, b, off + k + c0:off + k + c0 + rc, :] * taps[k]
                yc_ref[lb, b, c0:c0 + rc, :] = acc
        return carry

    lax.fori_loop(0, n_lb, lane_block, 0)

    acc = jnp.concatenate([yc_ref[lb] for lb in range(n_lb)], axis=-1).reshape(rows, e)
    mu = jnp.mean(acc, axis=-1, keepdims=True)
    xc = acc - mu
    var = jnp.mean(xc * xc, axis=-1, keepdims=True)
    y = _silu(xc * lax.rsqrt(var + EPS) * lng_ref[...] + lnb_ref[...])
    gate = _mm(h, win_ref[:, 2 * e:3 * e])
    y = (y * _silu(gate)).astype(BF16)
    xo_ref[...] = (x + _mm(y, wout_ref[...])).reshape(nb, tb, D_MODEL)

    for lb in range(n_lb):
        st_ref[0, :, :, lb * LANES:(lb + 1) * LANES] = (
            ext_ref[lb, :, HALO + tb - CONV_STATE:HALO + tb, :])
        tail = ext_ref[lb, :, tb:HALO + tb, :]
        ext_ref[lb, :, 0:HALO, :] = tail


def _conv_layer(x, buf, g, w_in, w_dw, b_dw, ln_g, ln_b, w_out, *, nb, tb):
    bsz, t_len, d = x.shape
    e = BRANCH
    n_lb = e // LANES
    has_buf = buf is not None
    grid = (bsz // nb, t_len // tb)
    xspec = pl.BlockSpec((nb, tb, d), lambda b, t: (b, t, 0))
    stspec = pl.BlockSpec((1, nb, CONV_STATE, e), lambda b, t: (0, b, 0, 0))
    in_specs = [xspec]
    args = [x]
    if has_buf:
        in_specs.append(stspec)
        args.append(buf)
    in_specs += [_const_spec((1, d)), _const_spec((d, 3 * e)),
                 _const_spec((n_lb, CONV_WIDTH, LANES)), _const_spec((n_lb, 1, LANES)),
                 _const_spec((1, e)), _const_spec((1, e)), _const_spec((e, d))]
    w_dw_lb = w_dw.reshape(CONV_WIDTH, n_lb, LANES).transpose(1, 0, 2)
    args += [g.reshape(1, d), w_in.astype(BF16), w_dw_lb, b_dw.reshape(n_lb, 1, LANES),
             ln_g.reshape(1, e), ln_b.reshape(1, e), w_out.astype(BF16)]
    return pl.pallas_call(
        functools.partial(_conv_body, nb=nb, tb=tb, has_buf=has_buf),
        grid=grid,
        in_specs=in_specs,
        out_specs=[xspec, stspec],
        out_shape=[jax.ShapeDtypeStruct(x.shape, F32),
                   jax.ShapeDtypeStruct((1, bsz, CONV_STATE, e), F32)],
        scratch_shapes=[pltpu.VMEM((n_lb, nb, HALO + tb, LANES), F32),
                        pltpu.VMEM((n_lb, nb, tb, LANES), F32)],
        compiler_params=_params(2),
        name="conv_layer",
    )(*args)


def _head_slice(hd):
    return slice(hd * SB_HEAD_DIM, (hd + 1) * SB_HEAD_DIM)


def _sbproj_body(*refs, nb, tb, with_bf16_kv):
    if with_bf16_kv:
        (x_ref, g_ref, win_ref, qg_ref, kg_ref,
         q_ref, k_ref, v_ref, sg_ref, kb_ref, vb_ref) = refs
    else:
        (x_ref, g_ref, win_ref, qg_ref, kg_ref,
         q_ref, k_ref, v_ref, sg_ref) = refs
    e, dh = BRANCH, SB_HEAD_DIM
    rows = nb * tb
    x = x_ref[...].reshape(rows, D_MODEL)
    h = _rms(x, g_ref[...]).astype(BF16)
    q = _mm(h, win_ref[:, 0:e])
    for hd in range(SB_HEADS):
        qn = _rms(q[:, _head_slice(hd)], qg_ref[...]) * Q_SCALE
        q_ref[:, :, _head_slice(hd)] = qn.astype(BF16).reshape(nb, tb, dh)
    k = _mm(h, win_ref[:, e:2 * e])
    for hd in range(SB_HEADS):
        kn = _rms(k[:, _head_slice(hd)], kg_ref[...])
        k_ref[0, :, :, hd, :] = kn.reshape(nb, tb, dh)
        if with_bf16_kv:
            kb_ref[:, :, _head_slice(hd)] = kn.astype(BF16).reshape(nb, tb, dh)
    v = _mm(h, win_ref[:, 2 * e:3 * e])
    for hd in range(SB_HEADS):
        v_ref[0, :, :, hd, :] = v[:, _head_slice(hd)].reshape(nb, tb, dh)
    if with_bf16_kv:
        vb_ref[...] = v.astype(BF16).reshape(nb, tb, e)
    sg_ref[...] = _silu(_mm(h, win_ref[:, 3 * e:4 * e])).reshape(nb, tb, e)


def _sb_proj(x, g, w_in, q_g, k_g, *, nb, tb, with_bf16_kv):
    bsz, t_len, d = x.shape
    e = BRANCH
    xspec = pl.BlockSpec((nb, tb, e), lambda b, t: (b, t, 0))
    hspec = pl.BlockSpec((1, nb, tb, SB_HEADS, SB_HEAD_DIM), lambda b, t: (0, b, t, 0, 0))
    tok = (bsz, t_len, e)
    state = (1, bsz, t_len, SB_HEADS, SB_HEAD_DIM)
    out_specs = [xspec, hspec, hspec, xspec]
    out_shape = [jax.ShapeDtypeStruct(tok, BF16), jax.ShapeDtypeStruct(state, F32),
                 jax.ShapeDtypeStruct(state, F32), jax.ShapeDtypeStruct(tok, F32)]
    if with_bf16_kv:
        out_specs += [xspec, xspec]
        out_shape += [jax.ShapeDtypeStruct(tok, BF16)] * 2
    return pl.pallas_call(
        functools.partial(_sbproj_body, nb=nb, tb=tb, with_bf16_kv=with_bf16_kv),
        grid=(bsz // nb, t_len // tb),
        in_specs=[xspec, _const_spec((1, d)), _const_spec((d, 4 * e)),
                  _const_spec((1, SB_HEAD_DIM)), _const_spec((1, SB_HEAD_DIM))],
        out_specs=out_specs,
        out_shape=out_shape,
        compiler_params=_params(2),
        name="sb_proj",
    )(x, g.reshape(1, d), w_in.astype(BF16), q_g.reshape(1, -1), k_g.reshape(1, -1))


def _log2_one_minus_beta(n):
    sign_bit = jnp.int32(-2 ** 31)
    neg_abs = lax.bitcast_convert_type(lax.bitcast_convert_type(n, jnp.int32) | sign_bit, F32)
    soft = jnp.log(1.0 + jnp.exp2(neg_abs)) * LOG2E
    return jnp.minimum(n, 0.0) - soft


def _sbattn_body(q_ref, kb_ref, vb_ref, sg_ref, x_ref, wout_ref, xo_ref, acc_ref, c_ref, *, tq):
    i = pl.program_id(1)
    tk = tq
    row = lax.broadcasted_iota(jnp.int32, (tq, tk), 0)
    col = lax.broadcasted_iota(jnp.int32, (tq, tk), 1)
    mask = col < row
    tri = jnp.where(row >= col, 1.0, 0.0).astype(BF16)

    def step(s0, first):
        heads = [_head_slice(hd) for hd in range(SB_HEADS)]
        ns = [_mm_nt(q_ref[0, :, sl], kb_ref[0, pl.ds(s0, tk), sl]) for sl in heads]
        l1ms = []
        for n in ns:
            l1m = _log2_one_minus_beta(n)
            if first:
                l1m = jnp.where(mask, l1m, 0.0)
            l1ms.append(l1m)
        ss = [_mm(l1m.astype(BF16), tri) for l1m in l1ms]
        avs = []
        for sl, n, l1m, s in zip(heads, ns, l1ms, ss):
            tot = jnp.broadcast_to(jnp.sum(l1m, axis=-1, keepdims=True), (tq, LANES))
            if not first:
                c = c_ref[:, sl]
                s = s + jnp.concatenate([c] * (tk // LANES), axis=1)
                tot = tot + c
            c_ref[:, sl] = tot
            a = jnp.exp2(s - n)
            if first:
                a = jnp.where(mask, a, 0.0)
            avs.append(a.astype(BF16))
        for sl, a in zip(heads, avs):
            pv = _mm(a, vb_ref[0, pl.ds(s0, tk), sl])
            if first:
                acc_ref[:, sl] = pv
            else:
                acc_ref[:, sl] += pv

    step(pl.multiple_of(i * tq, tq), True)

    def body(jj, carry):
        step(pl.multiple_of((i - 1 - jj) * tk, tk), False)
        return carry

    lax.fori_loop(0, i, body, 0)
    y = (acc_ref[...] * sg_ref[0]).astype(BF16)
    xo_ref[0] = x_ref[0] + _mm(y, wout_ref[...])


def _sb_attn_prompt(q, kb, vb, sg, x, w_out, *, tq):
    bsz, t_len, e = q.shape
    d = x.shape[-1]
    tspec = pl.BlockSpec((1, tq, e), lambda b, i: (b, i, 0))
    kvspec = pl.BlockSpec((1, t_len, e), lambda b, i: (b, 0, 0))
    return pl.pallas_call(
        functools.partial(_sbattn_body, tq=tq),
        grid=(bsz, t_len // tq),
        in_specs=[tspec, kvspec, kvspec, tspec, tspec, _const_spec((e, d))],
        out_specs=tspec,
        out_shape=jax.ShapeDtypeStruct(x.shape, F32),
        scratch_shapes=[pltpu.VMEM((tq, e), F32), pltpu.VMEM((tq, e), F32)],
        compiler_params=_params(2),
        name="sb_attn_prompt",
    )(q, kb, vb, sg, x, w_out.astype(BF16))


def _heads_to_lanes(ref5):
    return jnp.concatenate([ref5[0, 0, :, hd, :] for hd in range(SB_HEADS)], axis=-1)


def _sbattn_sample_body(q_ref, kn_ref, vn_ref, kc_ref, vc_ref, sg_ref, x_ref, wout_ref,
                        xo_ref, qbd_ref, c_ref, acc_ref, *, t_new, n_steps):
    s = pl.program_id(1)
    e, dh = BRANCH, SB_HEAD_DIM
    lanes = SB_HEADS * t_new

    def block(kblk, vblk, mask):
        tk = kblk.shape[0]
        n = _mm_nt(kblk, qbd_ref[...])
        l1m = _log2_one_minus_beta(n)
        if mask is not None:
            l1m = jnp.where(mask, l1m, 0.0)
        tr = lax.broadcasted_iota(jnp.int32, (tk, tk), 0)
        tc = lax.broadcasted_iota(jnp.int32, (tk, tk), 1)
        tri = jnp.where(tc >= tr, 1.0, 0.0).astype(BF16)
        sfx = _mm(tri, l1m.astype(BF16)) + c_ref[0:1, :]
        a = jnp.exp2(sfx - n)
        if mask is not None:
            a = jnp.where(mask, a, 0.0)
        acc_ref[...] += _mm(a.T.astype(BF16), vblk)
        c_ref[...] = c_ref[...] + jnp.sum(l1m, axis=0, keepdims=True)

    @pl.when(s == 0)
    def _first():
        q8 = jnp.concatenate([q_ref[0]] * SB_HEADS, axis=0)
        rh = lax.broadcasted_iota(jnp.int32, (lanes, e), 0) // t_new
        ch = lax.broadcasted_iota(jnp.int32, (lanes, e), 1) // dh
        qbd_ref[...] = jnp.where(rh == ch, q8, jnp.zeros_like(q8))
        c_ref[...] = jnp.zeros_like(c_ref)
        acc_ref[...] = jnp.zeros_like(acc_ref)
        pad = jnp.zeros((LANES - t_new, e), F32)
        kblk = jnp.concatenate([_heads_to_lanes(kn_ref), pad], axis=0).astype(BF16)
        vblk = jnp.concatenate([_heads_to_lanes(vn_ref), pad], axis=0).astype(BF16)
        key = lax.broadcasted_iota(jnp.int32, (LANES, lanes), 0)
        qry = lax.broadcasted_iota(jnp.int32, (LANES, lanes), 1) % t_new
        block(kblk, vblk, key < qry)

    @pl.when(s > 0)
    def _cached():
        block(_heads_to_lanes(kc_ref).astype(BF16), _heads_to_lanes(vc_ref).astype(BF16), None)

    @pl.when(s == n_steps - 1)
    def _last():
        parts = [acc_ref[hd * t_new:(hd + 1) * t_new, _head_slice(hd)]
                 for hd in range(SB_HEADS)]
        y = (jnp.concatenate(parts, axis=1) * sg_ref[0]).astype(BF16)
        xo_ref[0] = x_ref[0] + _mm(y, wout_ref[...])


def _sb_attn_sample(q, k_new, v_new, cache_k, cache_v, sg, x, w_out, *, tkc):
    bsz, t_new, e = q.shape
    d = x.shape[-1]
    past = cache_k.shape[2]
    n_kv = past // tkc
    n_steps = n_kv + 1
    lanes = SB_HEADS * t_new
    assert lanes == LANES and past % tkc == 0
    tspec = pl.BlockSpec((1, t_new, e), lambda b, s: (b, 0, 0))
    nspec = pl.BlockSpec((1, 1, t_new, SB_HEADS, SB_HEAD_DIM), lambda b, s: (0, b, 0, 0, 0))
    cspec = pl.BlockSpec((1, 1, tkc, SB_HEADS, SB_HEAD_DIM),
                         lambda b, s: (0, b, n_kv - jnp.maximum(s, 1), 0, 0))
    return pl.pallas_call(
        functools.partial(_sbattn_sample_body, t_new=t_new, n_steps=n_steps),
        grid=(bsz, n_steps),
        in_specs=[tspec, nspec, nspec, cspec, cspec, tspec, tspec, _const_spec((e, d))],
        out_specs=tspec,
        out_shape=jax.ShapeDtypeStruct(x.shape, F32),
        scratch_shapes=[pltpu.VMEM((lanes, e), BF16),
                        pltpu.VMEM((8, lanes), F32),
                        pltpu.VMEM((lanes, e), F32)],
        compiler_params=_params(2),
        name="sb_attn_sample",
    )(q, k_new, v_new, cache_k, cache_v, sg, x, w_out.astype(BF16))


def _sb_layer_prompt(x, g, w_in, q_g, k_g, w_out, *, tb=512, tq=256):
    q, k, v, sg, kb, vb = _sb_proj(x, g, w_in, q_g, k_g, nb=1, tb=tb, with_bf16_kv=True)
    return _sb_attn_prompt(q, kb, vb, sg, x, w_out, tq=tq), k, v


def _sb_layer_sample(x, cache_k, cache_v, g, w_in, q_g, k_g, w_out, *, tkc=512):
    bsz, t_len, _ = x.shape
    q, k, v, sg = _sb_proj(x, g, w_in, q_g, k_g, nb=bsz, tb=t_len, with_bf16_kv=False)
    return _sb_attn_sample(q, k, v, cache_k, cache_v, sg, x, w_out, tkc=tkc), k, v


def _gmlp_body(*refs, nb, tb, length, with_gv):
    if with_gv:
        (x_ref, g_ref, win_ref, vg_ref, ws_ref, bst_ref, wout_ref, fg_ref,
         yo_ref, gv_ref) = refs
    else:
        (x_ref, g_ref, win_ref, vg_ref, ws_ref, bst_ref, wout_ref, fg_ref, yo_ref) = refs
    rows = nb * tb
    e, mg = BRANCH, MLP_GROUP
    x = x_ref[...].reshape(rows, D_MODEL)
    h = _rms(x, g_ref[...]).astype(BF16)
    u = _mm(h, win_ref[:, 0:e])
    vv = _rms(_mm(h, win_ref[:, e:2 * e]), vg_ref[...])
    gate = _mm(h, win_ref[:, 2 * e:3 * e])
    if with_gv:
        gv_ref[0] = vv.reshape(nb, tb, e)
    vb = vv.astype(BF16)
    tr = lax.broadcasted_iota(jnp.int32, (length, length), 0)
    tc = lax.broadcasted_iota(jnp.int32, (length, length), 1)
    cols = []
    for gi in range(MLP_GROUPS):
        w = jnp.where(tr >= tc, ws_ref[gi, 0:length, 0:length], 0.0).astype(BF16)
        b = bst_ref[0:length, gi:gi + 1]
        chunks = []
        for r in range(rows // length):
            vch = vb[r * length:(r + 1) * length, gi * mg:(gi + 1) * mg]
            chunks.append(_mm(w, vch) + b)
        cols.append(jnp.concatenate(chunks, axis=0))
    y = u * jnp.concatenate(cols, axis=1)
    y = (y * _silu(gate)).astype(BF16)
    xo = x + _mm(y, wout_ref[...])
    yo_ref[...] = _rms(xo, fg_ref[...]).reshape(nb, tb, D_MODEL)


def _gmlp_layer(x, g, w_in, v_g, w_s, b_s, w_out, final_g, *, nb, tb, with_gv):
    bsz, t_len, d = x.shape
    e = BRANCH
    length = min(t_len, MLP_CHUNK)
    assert tb % length == 0
    xspec = pl.BlockSpec((nb, tb, d), lambda b, t: (b, t, 0))
    out_specs = [xspec]
    out_shape = [jax.ShapeDtypeStruct(x.shape, F32)]
    if with_gv:
        out_specs.append(pl.BlockSpec((1, nb, tb, e), lambda b, t: (0, b, t, 0)))
        out_shape.append(jax.ShapeDtypeStruct((1, bsz, t_len, e), F32))
    return pl.pallas_call(
        functools.partial(_gmlp_body, nb=nb, tb=tb, length=length, with_gv=with_gv),
        grid=(bsz // nb, t_len // tb),
        in_specs=[xspec, _const_spec((1, d)), _const_spec((d, 3 * e)), _const_spec((1, e)),
                  _const_spec((MLP_GROUPS, MLP_CHUNK, MLP_CHUNK)),
                  _const_spec((MLP_CHUNK, MLP_GROUPS)), _const_spec((e, d)),
                  _const_spec((1, d))],
        out_specs=out_specs,
        out_shape=out_shape,
        compiler_params=_params(2),
        name="gmlp_layer",
    )(x, g.reshape(1, d), w_in.astype(BF16), v_g.reshape(1, e), w_s, b_s.T,
      w_out.astype(BF16), final_g.reshape(1, d))


def kernel(x_prompt, x_sample, cache_pool, cache_conv, cache_sb_k, cache_sb_v, norm_g, final_g,
           a_w_in, a_w_grp, a_scale, a_w_out,
           b_w_in, b_w_dw, b_b_dw, b_ln_g, b_ln_b, b_w_out,
           c_w_in, c_q_g, c_k_g, c_w_out,
           d_w_in, d_v_g, d_w_s, d_b_s, d_w_out):
    assert norm_g.shape[0] == 4 and a_w_in.shape[0] == 1
    bs, ts, _ = x_sample.shape
    past = cache_sb_k.shape[2]
    tbp = 512

    def run(x, pool_buf, conv_buf, sb_k, sb_v, start, nb, tb):
        x, pool_st = _pool_layer(x, pool_buf, norm_g[0], a_w_in[0], a_w_grp[0], a_scale[0],
                                 a_w_out[0], nb=nb, tb=tb, start_pos=start)
        x, conv_st = _conv_layer(x, conv_buf, norm_g[1], b_w_in[0], b_w_dw[0], b_b_dw[0],
                                 b_ln_g[0], b_ln_b[0], b_w_out[0], nb=nb, tb=tb)
        prompt = sb_k is None
        if prompt:
            x, new_k, new_v = _sb_layer_prompt(x, norm_g[2], c_w_in[0], c_q_g[0], c_k_g[0],
                                               c_w_out[0], tb=tb)
        else:
            x, new_k, new_v = _sb_layer_sample(x, sb_k, sb_v, norm_g[2], c_w_in[0], c_q_g[0],
                                               c_k_g[0], c_w_out[0])
        outs = _gmlp_layer(x, norm_g[3], d_w_in[0], d_v_g[0], d_w_s[0], d_b_s[0], d_w_out[0],
                           final_g, nb=nb, tb=tb, with_gv=not prompt)
        return outs, pool_st, conv_st, new_k, new_v

    (y_p,), p_pool, p_conv, p_k, p_v = run(x_prompt, None, None, None, None, 0, 1, tbp)
    (y_s, s_gv), s_pool, s_conv, s_k, s_v = run(x_sample, cache_pool, cache_conv,
                                                cache_sb_k, cache_sb_v, past, bs, ts)
    return (y_p, y_s, p_pool, s_pool, p_conv, s_conv, p_k, p_v, s_k, s_v, s_gv)
```
